```python
import math
import jax, jax.numpy as jnp
from jax import lax
import numpy as np

D_MODEL = 2048
BATCH = 2
SEQ = 8192
DEPTH = 1
DEC_BATCH = 32
DEC_SEQ = 1
PAST_LEN = 16384
PAGE_SIZE = 128

RET_HEADS = 4
RET_DK = D_MODEL // 8
RET_DV = D_MODEL // 8
RET_CHUNK = 128
DIFF_HEADS = 8
DIFF_DK = D_MODEL // 32
DIFF_DV = 2 * DIFF_DK
ATTN_BLOCK = 128
MIX_WIDTH = RET_HEADS * RET_DV + DIFF_HEADS * DIFF_DV
IN_SIZES = (RET_HEADS * RET_DK, RET_HEADS * RET_DK, RET_HEADS * RET_DV, RET_HEADS * RET_DV,
            DIFF_HEADS * 2 * DIFF_DK, DIFF_HEADS * 2 * DIFF_DK, DIFF_HEADS * DIFF_DV)
IN_COLS = sum(IN_SIZES)
ROPE_THETA = 10000.0
MEM_LEN = 256
MEM_HEADS = 4
MEM_DH = D_MODEL // MEM_HEADS
PEER_HEADS = 8
PEER_NKEYS = 128
PEER_EXPERTS = PEER_NKEYS * PEER_NKEYS
PEER_DKEY = 256
PEER_TOPK = 16
PEER_BLOCK = 128
EPS = 1e-6

kernel_name = 'hymba_retnet_diffattn_peer_step'

F32 = jnp.float32


def rmsnorm(x, g=None):
    xf = x.astype(F32)
    y = xf * lax.rsqrt(jnp.mean(xf * xf, axis=-1, keepdims=True) + EPS)
    if g is not None:
        y = y * g.astype(F32)
    return y.astype(x.dtype)


def rope(x, pos):
    d = x.shape[-1]
    half = d // 2
    inv = ROPE_THETA ** (-jnp.arange(half, dtype=F32) / half)
    ang = pos[:, None] * inv[None, :]
    shape = (1, pos.shape[0]) + (1,) * (x.ndim - 3) + (half,)
    cos = jnp.cos(ang).reshape(shape)
    sin = jnp.sin(ang).reshape(shape)
    xf = x.astype(F32)
    x1, x2 = xf[..., :half], xf[..., half:]
    return jnp.concatenate([x1 * cos - x2 * sin, x2 * cos + x1 * sin], axis=-1).astype(x.dtype)


def in_projections(h, w_in, pos):
    B, L, _ = h.shape
    proj = h @ w_in
    split_at = [int(s) for s in np.cumsum(IN_SIZES)[:-1]]
    rq, rk, rv, rg, dq, dk, dv = jnp.split(proj, split_at, axis=-1)
    rq = rope(rq.reshape(B, L, RET_HEADS, RET_DK), pos)
    rk = rope(rk.reshape(B, L, RET_HEADS, RET_DK), pos) * (RET_DK ** -0.5)
    rv = rv.reshape(B, L, RET_HEADS, RET_DV)
    rg = rg.reshape(B, L, RET_HEADS, RET_DV)
    dq = rope(dq.reshape(B, L, DIFF_HEADS, 2, DIFF_DK), pos)
    dk = rope(dk.reshape(B, L, DIFF_HEADS, 2, DIFF_DK), pos)
    dv = dv.reshape(B, L, DIFF_HEADS, DIFF_DV)
    return rq, rk, rv, rg, dq, dk, dv


def retention(q, k, v, s0, chunk):
    B, L, H, dk = q.shape
    dv = v.shape[-1]
    n = L // chunk
    lg = jnp.log(1.0 - 2.0 ** (-5.0 - jnp.arange(H, dtype=F32)))
    idx = jnp.arange(chunk, dtype=F32)
    rel = idx[:, None] - idx[None, :]
    dmask = jnp.where(rel[None] >= 0, jnp.exp(jnp.maximum(rel, 0.0)[None] * lg[:, None, None]), 0.0)
    q_dec = jnp.exp((idx + 1.0)[:, None] * lg[None, :])
    k_dec = jnp.exp((chunk - 1.0 - idx)[:, None] * lg[None, :])
    s_dec = jnp.exp(chunk * lg)

    def to_chunks(t):
        return t.reshape((B, n, chunk) + t.shape[2:]).swapaxes(0, 1)

    def step(s, inp):
        qc, kc, vc = inp
        qf, kf, vf = qc.astype(F32), kc.astype(F32), vc.astype(F32)
        att = jnp.einsum('bihd,bjhd->bhij', qf, kf) * dmask[None]
        inner = jnp.einsum('bhij,bjhe->bihe', att, vf)
        cross = jnp.einsum('bihd,bhde->bihe', qf * q_dec[None, :, :, None], s)
        s_new = s * s_dec[None, :, None, None] + jnp.einsum('bjhd,bjhe->bhde', kf * k_dec[None, :, :, None], vf)
        return s_new, inner + cross

    s_fin, out = lax.scan(step, s0.astype(F32), (to_chunks(q), to_chunks(k), to_chunks(v)))
    out = out.swapaxes(0, 1).reshape(B, L, H, dv)
    return out, s_fin.astype(s0.dtype)


def diff_lambda_value(lam, lam_init):
    lam = lam.astype(F32)
    return jnp.exp(jnp.sum(lam[0] * lam[1])) - jnp.exp(jnp.sum(lam[2] * lam[3])) + lam_init


def diff_attn_prompt(q, k, v, lam):
    B, L, H, _, dk = q.shape
    nb = L // ATTN_BLOCK
    scale = dk ** -0.5
    kpos = jnp.arange(L)
    qb = q.reshape(B, nb, ATTN_BLOCK, H, 2, dk).swapaxes(0, 1)

    def block(args):
        qi, start = args
        s = jnp.einsum('bqhcd,bkhcd->bhcqk', qi, k).astype(F32) * scale
        qpos = start + jnp.arange(ATTN_BLOCK)
        s = jnp.where(kpos[None, :] <= qpos[:, None], s, -jnp.inf)
        p = jax.nn.softmax(s, axis=-1)
        w = p[:, :, 0] - lam * p[:, :, 1]
        return jnp.einsum('bhqk,bkhe->bqhe', w.astype(v.dtype), v)

    out = lax.map(block, (qb, jnp.arange(nb) * ATTN_BLOCK))
    return out.swapaxes(0, 1).reshape(B, L, H, v.shape[-1])


def diff_attn_sample(q, k_new, v_new, k_past, v_past, lam):
    T = q.shape[1]
    P = k_past.shape[1]
    scale = q.shape[-1] ** -0.5
    s_past = jnp.einsum('bqhcd,bkhcd->bhcqk', q, k_past).astype(F32) * scale
    s_new = jnp.einsum('bqhcd,bkhcd->bhcqk', q, k_new).astype(F32) * scale
    tri = jnp.arange(T)[None, :] <= jnp.arange(T)[:, None]
    s_new = jnp.where(tri, s_new, -jnp.inf)
    p = jax.nn.softmax(jnp.concatenate([s_past, s_new], axis=-1), axis=-1)
    w = (p[:, :, 0] - lam * p[:, :, 1]).astype(v_past.dtype)
    return (jnp.einsum('bhqk,bkhe->bqhe', w[..., :P], v_past)
            + jnp.einsum('bhqk,bkhe->bqhe', w[..., P:], v_new))


def merge_groups(ro, rg, do, subln, lam_init, w_out):
    B, L = ro.shape[:2]
    r = rmsnorm(ro) * jax.nn.silu(rg.astype(F32))
    d = rmsnorm(do, subln).astype(F32) * (1.0 - lam_init)
    cat = jnp.concatenate([r.reshape(B, L, -1), d.reshape(B, L, -1)], axis=-1).astype(w_out.dtype)
    return cat @ w_out


def mem_kv(mem, g, w_ck, w_cv):
    B, M, _ = mem.shape
    m = rmsnorm(mem, g)
    return (m @ w_ck).reshape(B, M, MEM_HEADS, MEM_DH), (m @ w_cv).reshape(B, M, MEM_HEADS, MEM_DH)


def cross_attn(h, mk, mv, w_cq, w_co):
    B, L, _ = h.shape
    q = (h @ w_cq).reshape(B, L, MEM_HEADS, MEM_DH)
    s = jnp.einsum('bqhd,bkhd->bhqk', q, mk).astype(F32) * (MEM_DH ** -0.5)
    p = jax.nn.softmax(s, axis=-1)
    o = jnp.einsum('bhqk,bkhd->bqhd', p.astype(mv.dtype), mv).reshape(B, L, D_MODEL)
    return o @ w_co


def peer(h, w_pq, sub_keys, expert_u, expert_v):
    B, L, D = h.shape
    T = B * L
    nb = -(-T // PEER_BLOCK)
    xt = jnp.pad(h.reshape(T, D), ((0, nb * PEER_BLOCK - T), (0, 0))).reshape(nb, PEER_BLOCK, D)
    K = PEER_TOPK

    def block(xb):
        q = (xb @ w_pq).reshape(PEER_BLOCK, PEER_HEADS, 2, PEER_DKEY // 2)
        s = jnp.einsum('thcd,hcnd->thcn', q, sub_keys).astype(F32)
        sv, si = lax.top_k(s, K)
        cand = (sv[:, :, 0, :, None] + sv[:, :, 1, None, :]).reshape(PEER_BLOCK, PEER_HEADS, K * K)
        cidx = (si[:, :, 0, :, None] * PEER_NKEYS + si[:, :, 1, None, :]).reshape(PEER_BLOCK, PEER_HEADS, K * K)
        fs, fp = lax.top_k(cand, K)
        eid = jnp.take_along_axis(cidx, fp, axis=-1)
        g = jax.nn.softmax(fs, axis=-1)
        u = expert_u[eid]
        a = jax.nn.gelu(jnp.einsum('thkd,td->thk', u, xb).astype(F32), approximate=False)
        vv = expert_v[eid]
        return jnp.einsum('thk,thkd->td', (g * a).astype(vv.dtype), vv)

    out = lax.map(block, xt).reshape(nb * PEER_BLOCK, D)[:T]
    return out.reshape(B, L, D)


def setup_inputs(seed: int = 0) -> dict:
    key = jax.random.key(seed)
    ks = jax.random.split(key, 32)
    n_pages = PAST_LEN // PAGE_SIZE
    n_pool = (5 * DEC_BATCH * n_pages + 3) // 4
    nrm = lambda k, shape, s: jax.random.normal(k, shape, F32) * s
    gain = lambda k, shape: 1.0 + 0.02 * jax.random.normal(k, shape, F32)
    page_table = jax.random.permutation(ks[0], n_pool)[:DEC_BATCH * n_pages].reshape(DEC_BATCH, n_pages).astype(jnp.int32)
    return {
        'x_prompt': nrm(ks[1], (BATCH, SEQ, D_MODEL), 1.0),
        'x_sample': nrm(ks[2], (DEC_BATCH, DEC_SEQ, D_MODEL), 1.0),
        'state_ret': nrm(ks[3], (DEPTH, DEC_BATCH, RET_HEADS, RET_DK, RET_DV), 0.5),
        'cache_diff_k': nrm(ks[4], (DEPTH, n_pool, PAGE_SIZE, DIFF_HEADS, 2, DIFF_DK), 1.0),
        'cache_diff_v': nrm(ks[5], (DEPTH, n_pool, PAGE_SIZE, DIFF_HEADS, DIFF_DV), 1.0),
        'cache_mem_k': nrm(ks[6], (DEPTH, DEC_BATCH, MEM_LEN, MEM_HEADS, MEM_DH), 1.0),
        'cache_mem_v': nrm(ks[7], (DEPTH, DEC_BATCH, MEM_LEN, MEM_HEADS, MEM_DH), 1.0),
        'page_table': page_table,
        'mem_prompt': nrm(ks[8], (BATCH, MEM_LEN, D_MODEL), 1.0),
        'norm1': gain(ks[9], (DEPTH, D_MODEL)),
        'w_in': nrm(ks[10], (DEPTH, D_MODEL, IN_COLS), D_MODEL ** -0.5),
        'diff_lambda': nrm(ks[11], (DEPTH, 4, DIFF_DK), 0.1),
        'diff_subln': gain(ks[12], (DEPTH, DIFF_DV)),
        'w_out': nrm(ks[13], (DEPTH, MIX_WIDTH, D_MODEL), MIX_WIDTH ** -0.5),
        'norm2': gain(ks[14], (DEPTH, D_MODEL)),
        'norm_mem': gain(ks[15], (DEPTH, D_MODEL)),
        'w_cq': nrm(ks[16], (DEPTH, D_MODEL, D_MODEL), D_MODEL ** -0.5),
        'w_ck': nrm(ks[17], (DEPTH, D_MODEL, D_MODEL), D_MODEL ** -0.5),
        'w_cv': nrm(ks[18], (DEPTH, D_MODEL, D_MODEL), D_MODEL ** -0.5),
        'w_co': nrm(ks[19], (DEPTH, D_MODEL, D_MODEL), D_MODEL ** -0.5),
        'norm3': gain(ks[20], (DEPTH, D_MODEL)),
        'w_pq': nrm(ks[21], (DEPTH, D_MODEL, PEER_HEADS * PEER_DKEY), D_MODEL ** -0.5),
        'peer_sub_keys': nrm(ks[22], (DEPTH, PEER_HEADS, 2, PEER_NKEYS, PEER_DKEY // 2), (PEER_DKEY // 2) ** -0.5),
        'peer_u': nrm(ks[23], (DEPTH, PEER_EXPERTS, D_MODEL), D_MODEL ** -0.5),
        'peer_v': nrm(ks[24], (DEPTH, PEER_EXPERTS, D_MODEL), PEER_HEADS ** -0.5),
        'norm_f': gain(ks[25], (D_MODEL,)),
    }


def reference(x_prompt, x_sample, state_ret, cache_diff_k, cache_diff_v, cache_mem_k, cache_mem_v,
              page_table, mem_prompt, norm1, w_in, diff_lambda, diff_subln, w_out, norm2, norm_mem,
              w_cq, w_ck, w_cv, w_co, norm3, w_pq, peer_sub_keys, peer_u, peer_v, norm_f):
    B, L = x_prompt.shape[:2]
    DB, T = x_sample.shape[:2]
    past = page_table.shape[1] * cache_diff_k.shape[2]
    pos_p = jnp.arange(L, dtype=F32)
    pos_s = jnp.arange(T, dtype=F32) + past
    ret_chunk = RET_CHUNK if L % RET_CHUNK == 0 else L
    xp, xs = x_prompt, x_sample
    p_ret, p_k, p_v, p_mk, p_mv, s_ret, s_k, s_v = [], [], [], [], [], [], [], []
    for l in range(DEPTH):
        lam_init = 0.8 - 0.6 * math.exp(-0.3 * l)
        lam = diff_lambda_value(diff_lambda[l], lam_init)

        hp = rmsnorm(xp, norm1[l])
        rq, rk, rv, rg, dq, dk, dv = in_projections(hp, w_in[l], pos_p)
        s0 = jnp.zeros((B, RET_HEADS, RET_DK, RET_DV), F32)
        ro, s_fin = retention(rq, rk, rv, s0, ret_chunk)
        do = diff_attn_prompt(dq, dk, dv, lam)
        xp = xp + merge_groups(ro, rg, do, diff_subln[l], lam_init, w_out[l])
        mk, mv = mem_kv(mem_prompt, norm_mem[l], w_ck[l], w_cv[l])
        xp = xp + cross_attn(rmsnorm(xp, norm2[l]), mk, mv, w_cq[l], w_co[l])
        xp = xp + peer(rmsnorm(xp, norm3[l]), w_pq[l], peer_sub_keys[l], peer_u[l], peer_v[l])
        p_ret.append(s_fin); p_k.append(dk); p_v.append(dv); p_mk.append(mk); p_mv.append(mv)

        hs = rmsnorm(xs, norm1[l])
        rq, rk, rv, rg, dq, dk, dv = in_projections(hs, w_in[l], pos_s)
        ro, s_new = retention(rq, rk, rv, state_ret[l], T)
        k_past = cache_diff_k[l, page_table].reshape(DB, past, DIFF_HEADS, 2, DIFF_DK)
        v_past = cache_diff_v[l, page_table].reshape(DB, past, DIFF_HEADS, DIFF_DV)
        do = diff_attn_sample(dq, dk, dv, k_past, v_past, lam)
        xs = xs + merge_groups(ro, rg, do, diff_subln[l], lam_init, w_out[l])
        xs = xs + cross_attn(rmsnorm(xs, norm2[l]), cache_mem_k[l], cache_mem_v[l], w_cq[l], w_co[l])
        xs = xs + peer(rmsnorm(xs, norm3[l]), w_pq[l], peer_sub_keys[l], peer_u[l], peer_v[l])
        s_ret.append(s_new); s_k.append(dk); s_v.append(dv)

    y_prompt = rmsnorm(xp, norm_f)
    y_sample = rmsnorm(xs, norm_f)
    return (y_prompt, y_sample, jnp.stack(p_ret), jnp.stack(p_k), jnp.stack(p_v), jnp.stack(p_mk),
            jnp.stack(p_mv), jnp.stack(s_ret), jnp.stack(s_k), jnp.stack(s_v))
```

```python
import functools
import math

import numpy as np
import jax
import jax.numpy as jnp
from jax import lax
from jax.experimental import pallas as pl
from jax.experimental.pallas import tpu as pltpu

F32 = jnp.float32
BF16 = jnp.bfloat16
EPS = 1e-6
ROPE_THETA = 10000.0

RET_HEADS = 4
RET_DK = 256
RET_DV = 256
RET_CHUNK = 128
DIFF_HEADS = 8
DIFF_DK = 64
DIFF_DV = 128
MEM_HEADS = 4
PEER_HEADS = 8
PEER_NKEYS = 128
PEER_TOPK = 16
SEG = 1024
N_SEG = 7

LANES = 128
SUBLANES = 8
VMEM_LIMIT_BYTES = 56 * 1024 * 1024
NEG_BIG = -1e30

ROW_TILE = 512
ATTN_TILE = 256
PEER_TOKENS = 8
TOPK_TOKENS = 128


def _cparams(*sem):
    return pltpu.CompilerParams(dimension_semantics=sem, vmem_limit_bytes=VMEM_LIMIT_BYTES)


def _rms_rows(x):
    return x * lax.rsqrt(jnp.mean(x * x, axis=-1, keepdims=True) + EPS)


def _proj_kernel(*refs, norm, residual):
    it = iter(refs)
    x_ref = next(it)
    g_ref = next(it) if norm else None
    w_ref = next(it)
    r_ref = next(it) if residual else None
    o_ref = next(it)
    h_ref = next(it)

    @pl.when(pl.program_id(1) == 0)
    def _():
        x = x_ref[...]
        if norm:
            x = _rms_rows(x) * g_ref[...]
        h_ref[...] = x.astype(BF16)

    acc = jnp.dot(h_ref[...], w_ref[...], preferred_element_type=F32)
    if residual:
        acc = acc + r_ref[...]
    o_ref[...] = acc


def _proj(x, w_bf16, gain=None, residual=None, name="proj"):
    t, d = x.shape
    n = w_bf16.shape[1]
    tm = min(ROW_TILE, t)
    tn = min(SEG, n)
    assert t % tm == 0 and n % tn == 0
    in_specs = [pl.BlockSpec((tm, d), lambda i, j: (i, 0))]
    args = [x]
    if gain is not None:
        in_specs.append(pl.BlockSpec((1, d), lambda i, j: (0, 0)))
        args.append(gain.reshape(1, d))
    in_specs.append(pl.BlockSpec((d, tn), lambda i, j: (0, j)))
    args.append(w_bf16)
    if residual is not None:
        in_specs.append(pl.BlockSpec((tm, tn), lambda i, j: (i, j)))
        args.append(residual)
    return pl.pallas_call(
        functools.partial(_proj_kernel, norm=gain is not None, residual=residual is not None),
        grid=(t // tm, n // tn),
        in_specs=in_specs,
        out_specs=pl.BlockSpec((tm, tn), lambda i, j: (i, j)),
        out_shape=jax.ShapeDtypeStruct((t, n), F32),
        scratch_shapes=[pltpu.VMEM((tm, d), BF16)],
        compiler_params=_cparams("parallel", "arbitrary"),
        name=name,
    )(*args)


def _inproj_kernel(x_ref, g_ref, w_ref, cr_ref, sr_ref, cd_ref, sd_ref, o_ref, h_ref):
    j = pl.program_id(1)

    @pl.when(j == 0)
    def _():
        h_ref[...] = (_rms_rows(x_ref[...]) * g_ref[...]).astype(BF16)

    acc = jnp.dot(h_ref[...], w_ref[...], preferred_element_type=F32)

    @pl.when(j < 2)
    def _():
        cos = cr_ref[...]
        sin = sr_ref[...]
        scale = jnp.where(j == 1, RET_DK ** -0.5, 1.0).astype(F32)
        for h in range(RET_HEADS):
            lo = h * RET_DK
            x1 = acc[:, lo:lo + LANES]
            x2 = acc[:, lo + LANES:lo + 2 * LANES]
            o_ref[:, lo:lo + LANES] = (x1 * cos - x2 * sin) * scale
            o_ref[:, lo + LANES:lo + 2 * LANES] = (x2 * cos + x1 * sin) * scale

    @pl.when((j == 4) | (j == 5))
    def _():
        cos = cd_ref[...]
        sin_signed = sd_ref[...]
        lane = lax.broadcasted_iota(jnp.int32, cos.shape, 1)
        first_half = (lane % DIFF_DK) < (DIFF_DK // 2)
        for c in range(SEG // LANES):
            xs = acc[:, c * LANES:(c + 1) * LANES]
            partner = jnp.where(first_half,
                                pltpu.roll(xs, LANES - DIFF_DK // 2, 1),
                                pltpu.roll(xs, DIFF_DK // 2, 1))
            o_ref[:, c * LANES:(c + 1) * LANES] = xs * cos + partner * sin_signed

    @pl.when((j == 2) | (j == 3) | (j == 6))
    def _():
        o_ref[...] = acc


def _rope_tables(pos):
    def cs(half):
        inv = ROPE_THETA ** (-jnp.arange(half, dtype=F32) / half)
        ang = pos[:, None] * inv[None, :]
        return jnp.cos(ang), jnp.sin(ang)
    cr, sr = cs(RET_DK // 2)
    cd, sd = cs(DIFF_DK // 2)
    cd = jnp.tile(cd, (1, LANES // (DIFF_DK // 2)))
    sd = jnp.tile(jnp.concatenate([-sd, sd], axis=1), (1, LANES // DIFF_DK))
    return cr, sr, cd, sd


def _inproj(x, gain, w_bf16, tables, name):
    t, d = x.shape
    tm = min(ROW_TILE, t)
    npos = tables[0].shape[0] // tm
    tab_spec = pl.BlockSpec((tm, LANES), lambda i, j: (i % npos, 0))
    return pl.pallas_call(
        _inproj_kernel,
        grid=(t // tm, N_SEG),
        in_specs=[pl.BlockSpec((tm, d), lambda i, j: (i, 0)),
                  pl.BlockSpec((1, d), lambda i, j: (0, 0)),
                  pl.BlockSpec((d, SEG), lambda i, j: (0, j)),
                  tab_spec, tab_spec, tab_spec, tab_spec],
        out_specs=pl.BlockSpec((tm, SEG), lambda i, j: (i, j)),
        out_shape=jax.ShapeDtypeStruct((t, N_SEG * SEG), F32),
        scratch_shapes=[pltpu.VMEM((tm, d), BF16)],
        compiler_params=_cparams("parallel", "arbitrary"),
        name=name,
    )(x, gain.reshape(1, d), w_bf16, *tables)


def _ret_log_decay():
    return np.log(1.0 - 2.0 ** (-5.0 - np.arange(RET_HEADS, dtype=np.float64)))


def _ret_tables(chunk):
    lg = _ret_log_decay()
    idx = np.arange(chunk, dtype=np.float64)
    rel = idx[:, None] - idx[None, :]
    dmask = np.where(rel[None] >= 0, np.exp(np.maximum(rel, 0.0)[None] * lg[:, None, None]), 0.0)
    q_dec = np.exp((idx + 1.0)[None, :] * lg[:, None])
    k_dec = np.exp((chunk - 1.0 - idx)[None, :] * lg[:, None])
    rep = lambda a: np.broadcast_to(a[:, :, None], (RET_HEADS, chunk, LANES))
    return (jnp.asarray(dmask, F32), jnp.asarray(rep(q_dec), F32), jnp.asarray(rep(k_dec), F32))


def _ret_kernel(q_ref, k_ref, v_ref, dm_ref, qd_ref, kd_ref, o_ref, sfin_ref, s_scr, *, chunk):
    c = pl.program_id(1)

    @pl.when(c == 0)
    def _():
        s_scr[...] = jnp.zeros_like(s_scr)

    s_dec = np.exp(chunk * _ret_log_decay())
    for h in range(RET_HEADS):
        cols = slice(h * RET_DK, (h + 1) * RET_DK)
        q = q_ref[:, cols]
        k = k_ref[:, cols]
        vb = v_ref[:, cols].astype(BF16)
        qd = jnp.concatenate([qd_ref[h], qd_ref[h]], axis=1)
        kd = jnp.concatenate([kd_ref[h], kd_ref[h]], axis=1)
        att = lax.dot_general(q.astype(BF16), k.astype(BF16), (((1,), (1,)), ((), ())),
                              preferred_element_type=F32) * dm_ref[h]
        inner = jnp.dot(att.astype(BF16), vb, preferred_element_type=F32)
        s = s_scr[h]
        cross = jnp.dot((q * qd).astype(BF16), s.astype(BF16), preferred_element_type=F32)
        kt = jnp.transpose(k * kd).astype(BF16)
        s_scr[h] = s * float(s_dec[h]) + jnp.dot(kt, vb, preferred_element_type=F32)
        o_ref[:, cols] = inner + cross

    @pl.when(c == pl.num_programs(1) - 1)
    def _():
        sfin_ref[0] = s_scr[...]


def _retention_prompt(proj, batch, seq):
    chunk = RET_CHUNK if seq % RET_CHUNK == 0 else seq
    n = seq // chunk
    dm, qd, kd = _ret_tables(chunk)
    row = lambda seg: pl.BlockSpec((chunk, SEG), lambda b, c: (b * n + c, seg))
    tab = lambda a: pl.BlockSpec(a.shape, lambda b, c: (0, 0, 0))
    return pl.pallas_call(
        functools.partial(_ret_kernel, chunk=chunk),
        grid=(batch, n),
        in_specs=[row(0), row(1), row(2), tab(dm), tab(qd), tab(kd)],
        out_specs=[pl.BlockSpec((chunk, SEG), lambda b, c: (b * n + c, 0)),
                   pl.BlockSpec((1, RET_HEADS, RET_DK, RET_DV), lambda b, c: (b, 0, 0, 0))],
        out_shape=[jax.ShapeDtypeStruct((batch * seq, RET_HEADS * RET_DV), F32),
                   jax.ShapeDtypeStruct((batch, RET_HEADS, RET_DK, RET_DV), F32)],
        scratch_shapes=[pltpu.VMEM((RET_HEADS, RET_DK, RET_DV), F32)],
        compiler_params=_cparams("parallel", "arbitrary"),
        name="retention_prompt",
    )(proj, proj, proj, dm, qd, kd)


def _ret_step_kernel(p_ref, s_ref, o_ref, snew_ref):
    b = pl.program_id(0)
    row = p_ref[pl.ds(b, 1), :]
    gamma = np.exp(_ret_log_decay())
    rnd = lambda a: a.astype(BF16).astype(F32)
    for h in range(RET_HEADS):
        q = row[:, h * RET_DK:(h + 1) * RET_DK]
        k = row[:, SEG + h * RET_DK:SEG + (h + 1) * RET_DK]
        v = row[:, 2 * SEG + h * RET_DV:2 * SEG + (h + 1) * RET_DV]
        kb, vb = rnd(k), rnd(v)
        s = s_ref[0, h]
        att = jnp.sum(rnd(q) * kb, axis=-1, keepdims=True)
        inner = rnd(att) * vb
        q8 = jnp.broadcast_to(q * float(gamma[h]), (SUBLANES, RET_DK)).astype(BF16)
        cross = jnp.dot(q8, s.astype(BF16), preferred_element_type=F32)[0:1]
        kcol = jnp.transpose(jnp.broadcast_to(kb, (LANES, RET_DK)))
        outer = jnp.concatenate([kcol * vb[:, :LANES], kcol * vb[:, LANES:]], axis=1)
        snew_ref[0, h] = s * float(gamma[h]) + outer
        o_ref[0, :, h * RET_DV:(h + 1) * RET_DV] = inner + cross


def _retention_step(proj, state):
    db = proj.shape[0]
    return pl.pallas_call(
        _ret_step_kernel,
        grid=(db,),
        in_specs=[pl.BlockSpec(proj.shape, lambda b: (0, 0)),
                  pl.BlockSpec((1, RET_HEADS, RET_DK, RET_DV), lambda b: (b, 0, 0, 0))],
        out_specs=[pl.BlockSpec((1, 1, RET_HEADS * RET_DV), lambda b: (b, 0, 0)),
                   pl.BlockSpec((1, RET_HEADS, RET_DK, RET_DV), lambda b: (b, 0, 0, 0))],
        out_shape=[jax.ShapeDtypeStruct((db, 1, RET_HEADS * RET_DV), F32),
                   jax.ShapeDtypeStruct(state.shape, F32)],
        compiler_params=_cparams("parallel"),
        name="retention_step",
    )(proj, state)


def _diff_lambda(dl_ref, lam_init):
    dl = dl_ref[...]
    a = jnp.sum(dl[0:1] * dl[1:2], axis=-1, keepdims=True)
    b = jnp.sum(dl[2:3] * dl[3:4], axis=-1, keepdims=True)
    return jnp.exp(a) - jnp.exp(b) + lam_init


def _diff_kernel(q_ref, k_ref, v_ref, dl_ref, o_ref, qs_scr, kb_scr, vb_scr, m_scr, l_scr, acc_scr,
                 *, tile, lam_init):
    qi = pl.program_id(2)

    @pl.when(qi == 0)
    def _():
        kb_scr[...] = k_ref[...].astype(BF16)
        vb_scr[...] = v_ref[...].astype(BF16)

    q = q_ref[...] * (DIFF_DK ** -0.5)
    lane = lax.broadcasted_iota(jnp.int32, q.shape, 1)
    qs_scr[0:tile] = jnp.where(lane < DIFF_DK, q, 0.0).astype(BF16)
    qs_scr[tile:2 * tile] = jnp.where(lane >= DIFF_DK, q, 0.0).astype(BF16)
    m_scr[...] = jnp.full_like(m_scr, NEG_BIG)
    l_scr[...] = jnp.zeros_like(l_scr)
    acc_scr[...] = jnp.zeros_like(acc_scr)

    def step(ki, diagonal):
        start = pl.multiple_of(ki * tile, tile)
        kb = kb_scr[pl.ds(start, tile), :]
        s = lax.dot_general(qs_scr[...], kb, (((1,), (1,)), ((), ())), preferred_element_type=F32)
        if diagonal:
            r = lax.broadcasted_iota(jnp.int32, s.shape, 0)
            c = lax.broadcasted_iota(jnp.int32, s.shape, 1)
            s = jnp.where(c <= jnp.where(r >= tile, r - tile, r), s, NEG_BIG)
        m_prev = m_scr[...]
        m_new = jnp.maximum(m_prev, jnp.max(s, axis=-1, keepdims=True))
        p = jnp.exp(s - m_new)
        alpha = jnp.exp(m_prev - m_new)
        l_scr[...] = alpha * l_scr[...] + jnp.sum(p, axis=-1, keepdims=True)
        acc_scr[...] = alpha * acc_scr[...] + jnp.dot(p.astype(BF16), vb_scr[pl.ds(start, tile), :],
                                                       preferred_element_type=F32)
        m_scr[...] = m_new

    def body(ki, carry):
        step(ki, False)
        return carry

    lax.fori_loop(0, qi, body, 0)
    step(qi, True)

    lam = _diff_lambda(dl_ref, lam_init)
    o = acc_scr[...] / l_scr[...]
    o_ref[...] = o[0:tile] - lam * o[tile:2 * tile]


def _diff_prompt(proj, dlam, batch, seq, lam_init):
    tile = min(ATTN_TILE, seq)
    assert seq % tile == 0
    nq = seq // tile
    col = lambda seg: seg * (SEG // LANES)
    kv_spec = lambda seg: pl.BlockSpec((seq, LANES), lambda b, h, i: (b, col(seg) + h))
    return pl.pallas_call(
        functools.partial(_diff_kernel, tile=tile, lam_init=lam_init),
        grid=(batch, DIFF_HEADS, nq),
        in_specs=[pl.BlockSpec((tile, LANES), lambda b, h, i: (b * nq + i, col(4) + h)),
                  kv_spec(5), kv_spec(6),
                  pl.BlockSpec(dlam.shape, lambda b, h, i: (0, 0))],
        out_specs=pl.BlockSpec((tile, LANES), lambda b, h, i: (b * nq + i, h)),
        out_shape=jax.ShapeDtypeStruct((batch * seq, DIFF_HEADS * DIFF_DV), F32),
        scratch_shapes=[pltpu.VMEM((2 * tile, LANES), BF16),
                        pltpu.VMEM((seq, LANES), BF16),
                        pltpu.VMEM((seq, LANES), BF16),
                        pltpu.VMEM((2 * tile, 1), F32),
                        pltpu.VMEM((2 * tile, 1), F32),
                        pltpu.VMEM((2 * tile, LANES), F32)],
        compiler_params=_cparams("parallel", "parallel", "arbitrary"),
        name="diff_attn_prompt",
    )(proj, proj, proj, dlam)


def _diff_step_kernel(pt_ref, p_ref, kc_ref, vc_ref, dl_ref, o_ref, qm_scr, m_scr, l_scr, acc_scr,
                      *, lam_init):
    del pt_ref
    b = pl.program_id(0)
    pg = pl.program_id(1)
    n_rows = 2 * DIFF_HEADS
    width = DIFF_HEADS * 2 * DIFF_DK
    rowi = lax.broadcasted_iota(jnp.int32, (n_rows, width), 0)
    coli = lax.broadcasted_iota(jnp.int32, (n_rows, width), 1)

    @pl.when(pg == 0)
    def _():
        q = p_ref[pl.ds(b, 1), 4 * SEG:5 * SEG] * (DIFF_DK ** -0.5)
        qm_scr[...] = jnp.where(coli // DIFF_DK == rowi, q, 0.0).astype(BF16)
        m_scr[...] = jnp.full_like(m_scr, NEG_BIG)
        l_scr[...] = jnp.zeros_like(l_scr)
        acc_scr[...] = jnp.zeros_like(acc_scr)

    def update(s, vals_bf16, pv):
        m_prev = m_scr[...]
        m_new = jnp.maximum(m_prev, jnp.max(s, axis=-1, keepdims=True))
        p = jnp.exp(s - m_new)
        alpha = jnp.exp(m_prev - m_new)
        l_scr[...] = alpha * l_scr[...] + jnp.sum(p, axis=-1, keepdims=True)
        acc_scr[...] = alpha * acc_scr[...] + pv(p.astype(BF16), vals_bf16)
        m_scr[...] = m_new

    s_page = lax.dot_general(qm_scr[...], kc_ref[0].astype(BF16), (((1,), (1,)), ((), ())),
                             preferred_element_type=F32)
    update(s_page, vc_ref[0].astype(BF16),
           lambda p, v: jnp.dot(p, v, preferred_element_type=F32))

    @pl.when(pg == pl.num_programs(1) - 1)
    def _():
        k_new = p_ref[pl.ds(b, 1), 5 * SEG:6 * SEG].astype(BF16).astype(F32)
        v_new = p_ref[pl.ds(b, 1), 6 * SEG:7 * SEG].astype(BF16)
        s_new = jnp.sum(qm_scr[...].astype(F32) * k_new, axis=-1, keepdims=True)
        update(s_new, v_new, lambda p, v: p.astype(F32) * v.astype(F32))
        lam = _diff_lambda(dl_ref, lam_init)
        first = (rowi % 2) == 0
        coef = jnp.where(first, 1.0, -lam) / l_scr[...]
        own = (coli // DIFF_DV) == (rowi // 2)
        o_ref[0] = jnp.sum(jnp.where(own, coef * acc_scr[...], 0.0), axis=0, keepdims=True)


def _diff_step(proj, k_pages, v_pages, page_table, dlam, lam_init):
    db = proj.shape[0]
    n_pages = page_table.shape[1]
    page = k_pages.shape[1]
    width = k_pages.shape[2]
    grid_spec = pltpu.PrefetchScalarGridSpec(
        num_scalar_prefetch=1,
        grid=(db, n_pages),
        in_specs=[pl.BlockSpec(proj.shape, lambda b, p, pt: (0, 0)),
                  pl.BlockSpec((1, page, width), lambda b, p, pt: (pt[b * n_pages + p], 0, 0)),
                  pl.BlockSpec((1, page, width), lambda b, p, pt: (pt[b * n_pages + p], 0, 0)),
                  pl.BlockSpec(dlam.shape, lambda b, p, pt: (0, 0))],
        out_specs=pl.BlockSpec((1, 1, DIFF_HEADS * DIFF_DV), lambda b, p, pt: (b, 0, 0)),
        scratch_shapes=[pltpu.VMEM((2 * DIFF_HEADS, width), BF16),
                        pltpu.VMEM((2 * DIFF_HEADS, 1), F32),
                        pltpu.VMEM((2 * DIFF_HEADS, 1), F32),
                        pltpu.VMEM((2 * DIFF_HEADS, DIFF_HEADS * DIFF_DV), F32)])
    return pl.pallas_call(
        functools.partial(_diff_step_kernel, lam_init=lam_init),
        grid_spec=grid_spec,
        out_shape=jax.ShapeDtypeStruct((db, 1, DIFF_HEADS * DIFF_DV), F32),
        compiler_params=_cparams("parallel", "arbitrary"),
        name="diff_attn_step",
    )(page_table.reshape(-1), proj, k_pages, v_pages, dlam)


def _merge_kernel(ro_ref, rg_ref, do_ref, sub_ref, x_ref, w_ref, o_ref, cat_scr, *, lam_init):
    for h in range(RET_HEADS):
        cols = slice(h * RET_DV, (h + 1) * RET_DV)
        g = rg_ref[:, cols]
        swish = g * (1.0 / (1.0 + jnp.exp(-g)))
        cat_scr[:, cols] = (_rms_rows(ro_ref[:, cols]) * swish).astype(BF16)
    base = RET_HEADS * RET_DV
    for h in range(DIFF_HEADS):
        cols = slice(h * DIFF_DV, (h + 1) * DIFF_DV)
        d = _rms_rows(do_ref[:, cols]) * sub_ref[...]
        cat_scr[:, base + h * DIFF_DV:base + (h + 1) * DIFF_DV] = (d * (1.0 - lam_init)).astype(BF16)
    o_ref[...] = x_ref[...] + jnp.dot(cat_scr[...], w_ref[...], preferred_element_type=F32)


def _merge_out(ro, proj, do, subln, x, w_bf16, lam_init, name):
    t, d = x.shape
    tm = min(ROW_TILE // 2, t)
    half = pl.BlockSpec((tm, SEG), lambda i: (i, 0))
    return pl.pallas_call(
        functools.partial(_merge_kernel, lam_init=lam_init),
        grid=(t // tm,),
        in_specs=[half,
                  pl.BlockSpec((tm, SEG), lambda i: (i, 3)),
                  half,
                  pl.BlockSpec((1, DIFF_DV), lambda i: (0, 0)),
                  pl.BlockSpec((tm, d), lambda i: (i, 0)),
                  pl.BlockSpec(w_bf16.shape, lambda i: (0, 0))],
        out_specs=pl.BlockSpec((tm, d), lambda i: (i, 0)),
        out_shape=jax.ShapeDtypeStruct((t, d), F32),
        scratch_shapes=[pltpu.VMEM((tm, d), BF16)],
        compiler_params=_cparams("parallel"),
        name=name,
    )(ro, proj, do, subln.reshape(1, DIFF_DV), x, w_bf16)


def _cross_heads(q, mk_ref, mv_ref, write):
    dh = q.shape[-1] // MEM_HEADS
    for h in range(MEM_HEADS):
        cols = slice(h * dh, (h + 1) * dh)
        s = lax.dot_general(q[:, cols].astype(BF16), mk_ref[0, :, cols].astype(BF16),
                            (((1,), (1,)), ((), ())), preferred_element_type=F32) * (dh ** -0.5)
        e = jnp.exp(s - jnp.max(s, axis=-1, keepdims=True))
        p = e / jnp.sum(e, axis=-1, keepdims=True)
        write(cols, jnp.dot(p.astype(BF16), mv_ref[0, :, cols].astype(BF16), preferred_element_type=F32))


def _cross_kernel(q_ref, mk_ref, mv_ref, o_ref):
    def write(cols, val):
        o_ref[:, cols] = val
    _cross_heads(q_ref[...], mk_ref, mv_ref, write)


def _cross_prompt(q, mk, mv, batch, seq):
    d = q.shape[1]
    tq = min(ROW_TILE, seq)
    n = seq // tq
    mem = pl.BlockSpec((1,) + mk.shape[1:], lambda b, i: (b, 0, 0))
    return pl.pallas_call(
        _cross_kernel,
        grid=(batch, n),
        in_specs=[pl.BlockSpec((tq, d), lambda b, i: (b * n + i, 0)), mem, mem],
        out_specs=pl.BlockSpec((tq, d), lambda b, i: (b * n + i, 0)),
        out_shape=jax.ShapeDtypeStruct(q.shape, F32),
        compiler_params=_cparams("parallel", "parallel"),
        name="cross_attn_prompt",
    )(q, mk, mv)


def _cross_step_kernel(q_ref, mk_ref, mv_ref, o_ref):
    b = pl.program_id(0)
    q = jnp.broadcast_to(q_ref[pl.ds(b, 1), :], (SUBLANES, q_ref.shape[1]))

    def write(cols, val):
        o_ref[0, :, cols] = val[0:1]
    _cross_heads(q, mk_ref, mv_ref, write)


def _cross_step(q, mk, mv):
    db, d = q.shape
    mem = pl.BlockSpec((1,) + mk.shape[1:], lambda b: (b, 0, 0))
    return pl.pallas_call(
        _cross_step_kernel,
        grid=(db,),
        in_specs=[pl.BlockSpec(q.shape, lambda b: (0, 0)), mem, mem],
        out_specs=pl.BlockSpec((1, 1, d), lambda b: (b, 0, 0)),
        out_shape=jax.ShapeDtypeStruct((db, 1, d), F32),
        compiler_params=_cparams("parallel"),
        name="cross_attn_step",
    )(q, mk, mv)


def _top_rows(x, k):
    rows = lax.broadcasted_iota(jnp.int32, x.shape, 0).astype(F32)
    n_rows = float(x.shape[0])
    vals, idxs = [], []
    for _ in range(k):
        m = jnp.max(x, axis=0, keepdims=True)
        idx = jnp.min(jnp.where(x == m, rows, n_rows), axis=0, keepdims=True)
        vals.append(m)
        idxs.append(idx)
        x = jnp.where(rows == idx, -jnp.inf, x)
    return vals, idxs


def _topk_kernel(q_ref, keys_ref, eid_ref, gate_ref):
    qb = q_ref[...].astype(BF16)
    k = PEER_TOPK
    eids, gates = [], []
    for h in range(PEER_HEADS):
        sv, si = [], []
        for c in range(2):
            col = (2 * h + c) * LANES
            st = lax.dot_general(keys_ref[2 * h + c], qb[:, col:col + LANES], (((1,), (1,)), ((), ())),
                                 preferred_element_type=F32)
            vals, idxs = _top_rows(st, k)
            sv.append(vals)
            si.append(idxs)
        v1 = jnp.concatenate(sv[1], axis=0)
        i1 = jnp.concatenate(si[1], axis=0)
        cand = jnp.concatenate([sv[0][a] + v1 for a in range(k)], axis=0)
        cidx = jnp.concatenate([si[0][a] * float(PEER_NKEYS) + i1 for a in range(k)], axis=0)
        rows = lax.broadcasted_iota(jnp.int32, cand.shape, 0).astype(F32)
        fs, fp = _top_rows(cand, k)
        eids += [jnp.max(jnp.where(rows == p, cidx, -1.0), axis=0, keepdims=True) for p in fp]
        fs = jnp.concatenate(fs, axis=0)
        e = jnp.exp(fs - jnp.max(fs, axis=0, keepdims=True))
        gates.append(e / jnp.sum(e, axis=0, keepdims=True))
    eid_t = jnp.concatenate(eids, axis=0)
    gate_t = jnp.concatenate(gates, axis=0)
    eid_ref[...] = jnp.transpose(eid_t).astype(jnp.int32)
    gate_ref[...] = jnp.transpose(gate_t)


def _peer_topk(qp, keys_bf16, name):
    t, d = qp.shape
    tm = TOPK_TOKENS
    assert t % tm == 0
    width = PEER_HEADS * PEER_TOPK
    out = pl.BlockSpec((tm, width), lambda i: (i, 0))
    return pl.pallas_call(
        _topk_kernel,
        grid=(t // tm,),
        in_specs=[pl.BlockSpec((tm, d), lambda i: (i, 0)),
                  pl.BlockSpec(keys_bf16.shape, lambda i: (0, 0, 0))],
        out_specs=[out, out],
        out_shape=[jax.ShapeDtypeStruct((t, width), jnp.int32), jax.ShapeDtypeStruct((t, width), F32)],
        compiler_params=_cparams("parallel"),
        name=name,
    )(qp, keys_bf16)


def _peer_kernel(eid_now_ref, eid_next_ref, x_ref, gate_ref, g3_ref, gf_ref, u_hbm, v_hbm, y_ref,
                 ubuf, vbuf, sem, w_scr, *, tokens):
    i = pl.program_id(0)
    n = pl.num_programs(0)
    slot = i % 2
    n_exp = PEER_HEADS * PEER_TOPK
    half = SUBLANES

    def row_copy(table, buf, eid_ref, s, t, k, which):
        return pltpu.make_async_copy(table.at[eid_ref[t, k]], buf.at[s, t, k], sem.at[which, s])

    def issue(eid_ref, s):
        for t in range(tokens):
            def body(k, carry):
                row_copy(u_hbm, ubuf, eid_ref, s, t, k, 0).start()
                row_copy(v_hbm, vbuf, eid_ref, s, t, k, 1).start()
                return carry
            lax.fori_loop(0, n_exp, body, 0, unroll=8)

    @pl.when(i == 0)
    def _():
        issue(eid_now_ref, 0)

    @pl.when(i + 1 < n)
    def _():
        issue(eid_next_ref, 1 - slot)

    pltpu.make_async_copy(ubuf.at[slot], ubuf.at[slot], sem.at[0, slot]).wait()
    pltpu.make_async_copy(vbuf.at[slot], vbuf.at[slot], sem.at[1, slot]).wait()

    x = x_ref[...]
    ssq = jnp.sum(jnp.sum(x * x, axis=2, keepdims=True), axis=1, keepdims=True)
    hn = x * lax.rsqrt(ssq / float(x.shape[1] * x.shape[2]) + EPS) * g3_ref[...]

    lane = lax.broadcasted_iota(jnp.int32, (n_exp, LANES), 1)
    act_t = jnp.zeros((n_exp, LANES), F32)
    for t in range(tokens):
        u = ubuf[slot, t]
        prod = u[:, 0:half, :] * hn[t, 0:half][None] + u[:, half:, :] * hn[t, half:][None]
        a_col = jnp.sum(jnp.sum(prod, axis=1), axis=-1, keepdims=True)
        act_t = jnp.where(lane == t, a_col, act_t)

    gate_t = jnp.transpose(jnp.concatenate([gate_ref[...]] * (LANES // tokens), axis=0))
    gelu = 0.5 * act_t * (1.0 + lax.erf(act_t * (2.0 ** -0.5)))
    w_t = gate_t * gelu

    for t in range(tokens):
        w_scr[...] = jnp.broadcast_to(w_t[:, t:t + 1], (n_exp, LANES))
        parts = [jnp.zeros((2 * half, LANES), F32) for _ in range(4)]
        for k in range(n_exp):
            parts[k % 4] = parts[k % 4] + w_scr[k:k + 1, :] * vbuf[slot, t, k]
        xo = x[t] + ((parts[0] + parts[1]) + (parts[2] + parts[3]))
        ms = jnp.sum(jnp.sum(xo * xo, axis=1, keepdims=True), axis=0, keepdims=True)
        y_ref[t] = xo * lax.rsqrt(ms / float(xo.shape[0] * xo.shape[1]) + EPS) * gf_ref[...]


def _peer_mix(x, eid, gate, g3, gf, u_slab, v_slab, name):
    t, d = x.shape
    tokens = PEER_TOKENS
    assert t % tokens == 0
    n = t // tokens
    slab = (d // LANES, LANES)
    n_exp = PEER_HEADS * PEER_TOPK
    x3 = x.reshape(t, *slab)
    tok_spec = pl.BlockSpec((tokens,) + slab, lambda i: (i, 0, 0))
    gain_spec = pl.BlockSpec(slab, lambda i: (0, 0))
    y = pl.pallas_call(
        functools.partial(_peer_kernel, tokens=tokens),
        grid=(n,),
        in_specs=[pl.BlockSpec((tokens, n_exp), lambda i: (i, 0), memory_space=pltpu.SMEM),
                  pl.BlockSpec((tokens, n_exp), lambda i: (jnp.minimum(i + 1, n - 1), 0),
                               memory_space=pltpu.SMEM),
                  tok_spec,
                  pl.BlockSpec((tokens, n_exp), lambda i: (i, 0)),
                  gain_spec, gain_spec,
                  pl.BlockSpec(memory_space=pl.ANY),
                  pl.BlockSpec(memory_space=pl.ANY)],
        out_specs=tok_spec,
        out_shape=jax.ShapeDtypeStruct(x3.shape, F32),
        scratch_shapes=[pltpu.VMEM((2, tokens, n_exp) + slab, F32),
                        pltpu.VMEM((2, tokens, n_exp) + slab, F32),
                        pltpu.SemaphoreType.DMA((2, 2)),
                        pltpu.VMEM((n_exp, LANES), F32)],
        compiler_params=_cparams("arbitrary"),
        name=name,
    )(eid, eid, x3, gate, g3.reshape(slab), gf.reshape(slab), u_slab, v_slab)
    return y.reshape(t, d)


def _peer(x, g3, gf, w_pq, keys_bf16, u_slab, v_slab, name):
    t = x.shape[0]
    qp = _proj(x, w_pq, gain=g3, name=name + "_query")
    pad = (-t) % TOPK_TOKENS
    if pad:
        qp = jnp.pad(qp, ((0, pad), (0, 0)))
    eid, gate = _peer_topk(qp, keys_bf16, name + "_topk")
    return _peer_mix(x, eid[:t], gate[:t], g3, gf, u_slab, v_slab, name + "_mix")


def kernel(x_prompt, x_sample, state_ret, cache_diff_k, cache_diff_v, cache_mem_k, cache_mem_v, page_table, mem_prompt, norm1, w_in, diff_lambda, diff_subln, w_out, norm2, norm_mem, w_cq, w_ck, w_cv, w_co, norm3, w_pq, peer_sub_keys, peer_u, peer_v, norm_f):
    B, L, D = x_prompt.shape
    DB, TS = x_sample.shape[:2]
    depth = w_in.shape[0]
    assert depth == 1 and TS == 1
    M = mem_prompt.shape[1]
    n_pool, page = cache_diff_k.shape[1:3]
    past = page_table.shape[1] * page
    n_experts = peer_u.shape[1]
    slab = (D // LANES, LANES)

    xp = x_prompt.reshape(B * L, D)
    xs = x_sample.reshape(DB * TS, D)
    tab_p = _rope_tables(jnp.arange(L, dtype=F32))
    tab_s = _rope_tables(jnp.broadcast_to(jnp.arange(TS, dtype=F32) + past, (DB * TS,)))

    l = 0
    lam_init = 0.8 - 0.6 * math.exp(-0.3 * l)
    w_in_b = w_in[l].astype(BF16)
    w_out_b = w_out[l].astype(BF16)
    w_cq_b, w_ck_b, w_cv_b, w_co_b = (w[l].astype(BF16) for w in (w_cq, w_ck, w_cv, w_co))
    w_pq_b = w_pq[l].astype(BF16)
    keys_b = peer_sub_keys[l].reshape(2 * PEER_HEADS, PEER_NKEYS, -1).astype(BF16)
    u_slab = peer_u[l].reshape(n_experts, *slab)
    v_slab = peer_v[l].reshape(n_experts, *slab)
    dlam = diff_lambda[l]

    proj_p = _inproj(xp, norm1[l], w_in_b, tab_p, "in_proj_prompt")
    ro_p, sfin_p = _retention_prompt(proj_p, B, L)
    do_p = _diff_prompt(proj_p, dlam, B, L, lam_init)
    xp = _merge_out(ro_p, proj_p, do_p, diff_subln[l], xp, w_out_b, lam_init, "merge_out_prompt")
    mem = mem_prompt.reshape(B * M, D)
    mk_p = _proj(mem, w_ck_b, gain=norm_mem[l], name="mem_k")
    mv_p = _proj(mem, w_cv_b, gain=norm_mem[l], name="mem_v")
    q_p = _proj(xp, w_cq_b, gain=norm2[l], name="cross_q_prompt")
    o_p = _cross_prompt(q_p, mk_p.reshape(B, M, D), mv_p.reshape(B, M, D), B, L)
    xp = _proj(o_p, w_co_b, residual=xp, name="cross_o_prompt")
    y_p = _peer(xp, norm3[l], norm_f, w_pq_b, keys_b, u_slab, v_slab, "peer_prompt")

    proj_s = _inproj(xs, norm1[l], w_in_b, tab_s, "in_proj_sample")
    ro_s, s_new = _retention_step(proj_s, state_ret[l])
    k_pages = cache_diff_k[l].reshape(n_pool, page, -1)
    v_pages = cache_diff_v[l].reshape(n_pool, page, -1)
    do_s = _diff_step(proj_s, k_pages, v_pages, page_table, dlam, lam_init)
    xs = _merge_out(ro_s.reshape(DB, -1), proj_s, do_s.reshape(DB, -1), diff_subln[l], xs, w_out_b,
                    lam_init, "merge_out_sample")
    q_s = _proj(xs, w_cq_b, gain=norm2[l], name="cross_q_sample")
    o_s = _cross_step(q_s, cache_mem_k[l].reshape(DB, M, D), cache_mem_v[l].reshape(DB, M, D))
    xs = _proj(o_s.reshape(DB, D), w_co_b, residual=xs, name="cross_o_sample")
    y_s = _peer(xs, norm3[l], norm_f, w_pq_b, keys_b, u_slab, v_slab, "peer_sample")

    seg = lambda p, s: p[:, s * SEG:(s + 1) * SEG]
    return (y_p.reshape(B, L, D),
            y_s.reshape(DB, TS, D),
            sfin_p[None],
            seg(proj_p, 5).reshape(1, B, L, DIFF_HEADS, 2, DIFF_DK),
            seg(proj_p, 6).reshape(1, B, L, DIFF_HEADS, DIFF_DV),
            mk_p.reshape(1, B, M, MEM_HEADS, D // MEM_HEADS),
            mv_p.reshape(1, B, M, MEM_HEADS, D // MEM_HEADS),
            s_new[None],
            seg(proj_s, 5).reshape(1, DB, TS, DIFF_HEADS, 2, DIFF_DK),
            seg(proj_s, 6).reshape(1, DB, TS, DIFF_HEADS, DIFF_DV))
```

```python
import functools
import math

import numpy as np
import jax
import jax.numpy as jnp
from jax import lax
from jax.experimental import pallas as pl
from jax.experimental.pallas import tpu as pltpu

F32 = jnp.float32
BF16 = jnp.bfloat16
EPS = 1e-6
ROPE_THETA = 10000.0

RET_HEADS = 4
RET_DK = 256
RET_DV = 256
RET_CHUNK = 128
DIFF_HEADS = 8
DIFF_DK = 64
DIFF_DV = 128
MEM_HEADS = 4
PEER_HEADS = 8
PEER_NKEYS = 128
PEER_TOPK = 16
SEG = 1024
N_SEG = 7

LANES = 128
SUBLANES = 8
VMEM_LIMIT_BYTES = 56 * 1024 * 1024
NEG_BIG = -1e30

ROW_TILE = 512
ATTN_TILE = 512
PAGES_PER_STEP = 4
PEER_TOKENS = 8
TOPK_TOKENS = 128


def _cparams(*sem):
    return pltpu.CompilerParams(dimension_semantics=sem, vmem_limit_bytes=VMEM_LIMIT_BYTES)


def _rms_rows(x):
    return x * lax.rsqrt(jnp.mean(x * x, axis=-1, keepdims=True) + EPS)


def _proj_kernel(*refs, norm, residual):
    it = iter(refs)
    x_ref = next(it)
    g_ref = next(it) if norm else None
    w_ref = next(it)
    r_ref = next(it) if residual else None
    o_ref = next(it)
    h_ref = next(it)

    @pl.when(pl.program_id(1) == 0)
    def _():
        x = x_ref[...]
        if norm:
            x = _rms_rows(x) * g_ref[...]
        h_ref[...] = x.astype(BF16)

    acc = jnp.dot(h_ref[...], w_ref[...], preferred_element_type=F32)
    if residual:
        acc = acc + r_ref[...]
    o_ref[...] = acc


def _proj(x, w_bf16, gain=None, residual=None, name="proj"):
    t, d = x.shape
    n = w_bf16.shape[1]
    tm = min(ROW_TILE, t)
    tn = min(SEG, n)
    assert t % tm == 0 and n % tn == 0
    in_specs = [pl.BlockSpec((tm, d), lambda i, j: (i, 0))]
    args = [x]
    if gain is not None:
        in_specs.append(pl.BlockSpec((1, d), lambda i, j: (0, 0)))
        args.append(gain.reshape(1, d))
    in_specs.append(pl.BlockSpec((d, tn), lambda i, j: (0, j)))
    args.append(w_bf16)
    if residual is not None:
        in_specs.append(pl.BlockSpec((tm, tn), lambda i, j: (i, j)))
        args.append(residual)
    return pl.pallas_call(
        functools.partial(_proj_kernel, norm=gain is not None, residual=residual is not None),
        grid=(t // tm, n // tn),
        in_specs=in_specs,
        out_specs=pl.BlockSpec((tm, tn), lambda i, j: (i, j)),
        out_shape=jax.ShapeDtypeStruct((t, n), F32),
        scratch_shapes=[pltpu.VMEM((tm, d), BF16)],
        compiler_params=_cparams("parallel", "arbitrary"),
        name=name,
    )(*args)


def _inproj_kernel(x_ref, g_ref, w_ref, cr_ref, sr_ref, cd_ref, sd_ref, o_ref, h_ref):
    j = pl.program_id(1)

    @pl.when(j == 0)
    def _():
        h_ref[...] = (_rms_rows(x_ref[...]) * g_ref[...]).astype(BF16)

    acc = jnp.dot(h_ref[...], w_ref[...], preferred_element_type=F32)

    @pl.when(j < 2)
    def _():
        cos = cr_ref[...]
        sin = sr_ref[...]
        scale = jnp.where(j == 1, RET_DK ** -0.5, 1.0).astype(F32)
        for h in range(RET_HEADS):
            lo = h * RET_DK
            x1 = acc[:, lo:lo + LANES]
            x2 = acc[:, lo + LANES:lo + 2 * LANES]
            o_ref[:, lo:lo + LANES] = (x1 * cos - x2 * sin) * scale
            o_ref[:, lo + LANES:lo + 2 * LANES] = (x2 * cos + x1 * sin) * scale

    @pl.when((j == 4) | (j == 5))
    def _():
        cos = cd_ref[...]
        sin_signed = sd_ref[...]
        lane = lax.broadcasted_iota(jnp.int32, cos.shape, 1)
        first_half = (lane % DIFF_DK) < (DIFF_DK // 2)
        for c in range(SEG // LANES):
            xs = acc[:, c * LANES:(c + 1) * LANES]
            partner = jnp.where(first_half,
                                pltpu.roll(xs, LANES - DIFF_DK // 2, 1),
                                pltpu.roll(xs, DIFF_DK // 2, 1))
            o_ref[:, c * LANES:(c + 1) * LANES] = xs * cos + partner * sin_signed

    @pl.when((j == 2) | (j == 3) | (j == 6))
    def _():
        o_ref[...] = acc


def _rope_tables(pos):
    def cs(half):
        inv = ROPE_THETA ** (-jnp.arange(half, dtype=F32) / half)
        ang = pos[:, None] * inv[None, :]
        return jnp.cos(ang), jnp.sin(ang)
    cr, sr = cs(RET_DK // 2)
    cd, sd = cs(DIFF_DK // 2)
    cd = jnp.tile(cd, (1, LANES // (DIFF_DK // 2)))
    sd = jnp.tile(jnp.concatenate([-sd, sd], axis=1), (1, LANES // DIFF_DK))
    return cr, sr, cd, sd


def _inproj(x, gain, w_bf16, tables, name):
    t, d = x.shape
    tm = min(ROW_TILE, tables[0].shape[0])
    assert t % tm == 0
    npos = tables[0].shape[0] // tm
    tab_spec = pl.BlockSpec((tm, LANES), lambda i, j: (i % npos, 0))
    return pl.pallas_call(
        _inproj_kernel,
        grid=(t // tm, N_SEG),
        in_specs=[pl.BlockSpec((tm, d), lambda i, j: (i, 0)),
                  pl.BlockSpec((1, d), lambda i, j: (0, 0)),
                  pl.BlockSpec((d, SEG), lambda i, j: (0, j)),
                  tab_spec, tab_spec, tab_spec, tab_spec],
        out_specs=pl.BlockSpec((tm, SEG), lambda i, j: (i, j)),
        out_shape=jax.ShapeDtypeStruct((t, N_SEG * SEG), F32),
        scratch_shapes=[pltpu.VMEM((tm, d), BF16)],
        compiler_params=_cparams("parallel", "arbitrary"),
        name=name,
    )(x, gain.reshape(1, d), w_bf16, *tables)


def _ret_log_decay():
    return np.log(1.0 - 2.0 ** (-5.0 - np.arange(RET_HEADS, dtype=np.float64)))


def _ret_tables(chunk):
    lg = _ret_log_decay()
    idx = np.arange(chunk, dtype=np.float64)
    rel = idx[:, None] - idx[None, :]
    dmask = np.where(rel[None] >= 0, np.exp(np.maximum(rel, 0.0)[None] * lg[:, None, None]), 0.0)
    q_dec = np.exp((idx + 1.0)[None, :] * lg[:, None])
    k_dec = np.exp((chunk - 1.0 - idx)[None, :] * lg[:, None])
    rep = lambda a: np.broadcast_to(a[:, :, None], (RET_HEADS, chunk, LANES))
    return (jnp.asarray(dmask, F32), jnp.asarray(rep(q_dec), F32), jnp.asarray(rep(k_dec), F32))


def _ret_kernel(q_ref, k_ref, v_ref, dm_ref, qd_ref, kd_ref, o_ref, sfin_ref, s_scr, *, chunk):
    c = pl.program_id(1)

    @pl.when(c == 0)
    def _():
        s_scr[...] = jnp.zeros_like(s_scr)

    s_dec = np.exp(chunk * _ret_log_decay())
    for h in range(RET_HEADS):
        cols = slice(h * RET_DK, (h + 1) * RET_DK)
        q = q_ref[:, cols]
        k = k_ref[:, cols]
        vb = v_ref[:, cols].astype(BF16)
        qd = jnp.concatenate([qd_ref[h], qd_ref[h]], axis=1)
        kd = jnp.concatenate([kd_ref[h], kd_ref[h]], axis=1)
        att = lax.dot_general(q.astype(BF16), k.astype(BF16), (((1,), (1,)), ((), ())),
                              preferred_element_type=F32) * dm_ref[h]
        inner = jnp.dot(att.astype(BF16), vb, preferred_element_type=F32)
        s = s_scr[h]
        cross = jnp.dot((q * qd).astype(BF16), s.astype(BF16), preferred_element_type=F32)
        kt = jnp.transpose(k * kd).astype(BF16)
        s_scr[h] = s * float(s_dec[h]) + jnp.dot(kt, vb, preferred_element_type=F32)
        o_ref[:, cols] = inner + cross

    @pl.when(c == pl.num_programs(1) - 1)
    def _():
        sfin_ref[0] = s_scr[...]


def _retention_prompt(proj, batch, seq):
    chunk = RET_CHUNK if seq % RET_CHUNK == 0 else seq
    n = seq // chunk
    dm, qd, kd = _ret_tables(chunk)
    row = lambda seg: pl.BlockSpec((chunk, SEG), lambda b, c: (b * n + c, seg))
    tab = lambda a: pl.BlockSpec(a.shape, lambda b, c: (0, 0, 0))
    return pl.pallas_call(
        functools.partial(_ret_kernel, chunk=chunk),
        grid=(batch, n),
        in_specs=[row(0), row(1), row(2), tab(dm), tab(qd), tab(kd)],
        out_specs=[pl.BlockSpec((chunk, SEG), lambda b, c: (b * n + c, 0)),
                   pl.BlockSpec((1, RET_HEADS, RET_DK, RET_DV), lambda b, c: (b, 0, 0, 0))],
        out_shape=[jax.ShapeDtypeStruct((batch * seq, RET_HEADS * RET_DV), F32),
                   jax.ShapeDtypeStruct((batch, RET_HEADS, RET_DK, RET_DV), F32)],
        scratch_shapes=[pltpu.VMEM((RET_HEADS, RET_DK, RET_DV), F32)],
        compiler_params=_cparams("parallel", "arbitrary"),
        name="retention_prompt",
    )(proj, proj, proj, dm, qd, kd)


def _ret_step_kernel(p_ref, s_ref, o_ref, snew_ref):
    b = pl.program_id(0)
    row = p_ref[pl.ds(b, 1), :]
    gamma = np.exp(_ret_log_decay())
    rnd = lambda a: a.astype(BF16).astype(F32)
    for h in range(RET_HEADS):
        q = row[:, h * RET_DK:(h + 1) * RET_DK]
        k = row[:, SEG + h * RET_DK:SEG + (h + 1) * RET_DK]
        v = row[:, 2 * SEG + h * RET_DV:2 * SEG + (h + 1) * RET_DV]
        kb, vb = rnd(k), rnd(v)
        s = s_ref[0, h]
        att = jnp.sum(rnd(q) * kb, axis=-1, keepdims=True)
        inner = rnd(att) * vb
        q8 = jnp.broadcast_to(q * float(gamma[h]), (SUBLANES, RET_DK)).astype(BF16)
        cross = jnp.dot(q8, s.astype(BF16), preferred_element_type=F32)[0:1]
        kcol = jnp.transpose(jnp.broadcast_to(kb, (LANES, RET_DK)))
        outer = jnp.concatenate([kcol * vb[:, :LANES], kcol * vb[:, LANES:]], axis=1)
        snew_ref[0, h] = s * float(gamma[h]) + outer
        o_ref[0, :, h * RET_DV:(h + 1) * RET_DV] = inner + cross


def _retention_step(proj, state):
    db = proj.shape[0]
    return pl.pallas_call(
        _ret_step_kernel,
        grid=(db,),
        in_specs=[pl.BlockSpec(proj.shape, lambda b: (0, 0)),
                  pl.BlockSpec((1, RET_HEADS, RET_DK, RET_DV), lambda b: (b, 0, 0, 0))],
        out_specs=[pl.BlockSpec((1, 1, RET_HEADS * RET_DV), lambda b: (b, 0, 0)),
                   pl.BlockSpec((1, RET_HEADS, RET_DK, RET_DV), lambda b: (b, 0, 0, 0))],
        out_shape=[jax.ShapeDtypeStruct((db, 1, RET_HEADS * RET_DV), F32),
                   jax.ShapeDtypeStruct(state.shape, F32)],
        compiler_params=_cparams("parallel"),
        name="retention_step",
    )(proj, state)


def _diff_lambda(dl_ref, lam_init):
    dl = dl_ref[...]
    a = jnp.sum(dl[0:1] * dl[1:2], axis=-1, keepdims=True)
    b = jnp.sum(dl[2:3] * dl[3:4], axis=-1, keepdims=True)
    return jnp.exp(a) - jnp.exp(b) + lam_init


def _diff_kernel(q_ref, k_ref, v_ref, dl_ref, o_ref, qs_scr, kb_scr, vb_scr, m_scr, l_scr, acc_scr,
                 *, tile, lam_init):
    qi = pl.program_id(2)
    n_lane_tiles = tile // LANES

    @pl.when(qi == 0)
    def _():
        kb_scr[...] = k_ref[...].astype(BF16)
        vb_scr[...] = v_ref[...].astype(BF16)

    q = q_ref[...] * (DIFF_DK ** -0.5)
    lane = lax.broadcasted_iota(jnp.int32, q.shape, 1)
    qs_scr[0:tile] = jnp.where(lane < DIFF_DK, q, 0.0).astype(BF16)
    qs_scr[tile:2 * tile] = jnp.where(lane >= DIFF_DK, q, 0.0).astype(BF16)
    m_scr[...] = jnp.full_like(m_scr, NEG_BIG)
    l_scr[...] = jnp.zeros_like(l_scr)
    acc_scr[...] = jnp.zeros_like(acc_scr)

    def step(ki, diagonal):
        start = pl.multiple_of(ki * tile, tile)
        kb = kb_scr[pl.ds(start, tile), :]
        s = lax.dot_general(qs_scr[...], kb, (((1,), (1,)), ((), ())), preferred_element_type=F32)
        if diagonal:
            r = lax.broadcasted_iota(jnp.int32, s.shape, 0)
            c = lax.broadcasted_iota(jnp.int32, s.shape, 1)
            s = jnp.where(c <= jnp.where(r >= tile, r - tile, r), s, NEG_BIG)
        cols = [s[:, c * LANES:(c + 1) * LANES] for c in range(n_lane_tiles)]
        m_prev = m_scr[...]
        m_new = jnp.maximum(m_prev, jnp.max(functools.reduce(jnp.maximum, cols), axis=-1, keepdims=True))
        alpha = jnp.exp(m_prev - m_new)
        ps = [jnp.exp(x - m_new) for x in cols]
        l_scr[...] = alpha * l_scr[...] + functools.reduce(jnp.add, ps)
        p = jnp.concatenate([x.astype(BF16) for x in ps], axis=1)
        acc_scr[...] = alpha * acc_scr[...] + jnp.dot(p, vb_scr[pl.ds(start, tile), :],
                                                       preferred_element_type=F32)
        m_scr[...] = m_new

    def body(ki, carry):
        step(ki, False)
        return carry

    lax.fori_loop(0, qi, body, 0)
    step(qi, True)

    lam = _diff_lambda(dl_ref, lam_init)
    o = acc_scr[...] / jnp.sum(l_scr[...], axis=-1, keepdims=True)
    o_ref[...] = o[0:tile] - lam * o[tile:2 * tile]


def _diff_prompt(proj, dlam, batch, seq, lam_init):
    tile = min(ATTN_TILE, seq)
    assert seq % tile == 0 and tile % LANES == 0
    nq = seq // tile
    col = lambda seg: seg * (SEG // LANES)
    kv_spec = lambda seg: pl.BlockSpec((seq, LANES), lambda b, h, i: (b, col(seg) + h))
    return pl.pallas_call(
        functools.partial(_diff_kernel, tile=tile, lam_init=lam_init),
        grid=(batch, DIFF_HEADS, nq),
        in_specs=[pl.BlockSpec((tile, LANES), lambda b, h, i: (b * nq + i, col(4) + h)),
                  kv_spec(5), kv_spec(6),
                  pl.BlockSpec(dlam.shape, lambda b, h, i: (0, 0))],
        out_specs=pl.BlockSpec((tile, LANES), lambda b, h, i: (b * nq + i, h)),
        out_shape=jax.ShapeDtypeStruct((batch * seq, DIFF_HEADS * DIFF_DV), F32),
        scratch_shapes=[pltpu.VMEM((2 * tile, LANES), BF16),
                        pltpu.VMEM((seq, LANES), BF16),
                        pltpu.VMEM((seq, LANES), BF16),
                        pltpu.VMEM((2 * tile, LANES), F32),
                        pltpu.VMEM((2 * tile, LANES), F32),
                        pltpu.VMEM((2 * tile, LANES), F32)],
        compiler_params=_cparams("parallel", "parallel", "arbitrary"),
        name="diff_attn_prompt",
    )(proj, proj, proj, dlam)


def _diff_step_kernel(pt_ref, p_ref, *refs, lam_init, pages):
    del pt_ref
    k_refs = refs[:pages]
    v_refs = refs[pages:2 * pages]
    dl_ref, rep_ref, o_ref, qm_scr, m_scr, l_scr, acc_scr = refs[2 * pages:]
    b = pl.program_id(0)
    pg = pl.program_id(1)
    n_rows = 2 * DIFF_HEADS
    width = DIFF_HEADS * 2 * DIFF_DK
    page = k_refs[0].shape[-1]
    row_head = lambda shape: lax.broadcasted_iota(jnp.int32, shape, 0) % DIFF_HEADS

    @pl.when(pg == 0)
    def _():
        q = p_ref[pl.ds(b, 1), 4 * SEG:5 * SEG] * (DIFF_DK ** -0.5)
        rowi = lax.broadcasted_iota(jnp.int32, (n_rows, width), 0)
        coli = lax.broadcasted_iota(jnp.int32, (n_rows, width), 1)
        own = (coli // DIFF_DK) == 2 * (rowi % DIFF_HEADS) + rowi // DIFF_HEADS
        qm_scr[...] = jnp.where(own, q, 0.0).astype(BF16)
        m_scr[...] = jnp.full_like(m_scr, NEG_BIG)
        l_scr[...] = jnp.zeros_like(l_scr)
        acc_scr[...] = jnp.zeros_like(acc_scr)

    def update(s, weighted_values):
        m_prev = m_scr[...]
        m_new = jnp.maximum(m_prev, jnp.max(s, axis=-1, keepdims=True))
        p = jnp.exp(s - m_new)
        alpha = jnp.exp(m_prev - m_new)
        l_scr[...] = alpha * l_scr[...] + jnp.sum(p, axis=-1, keepdims=True)
        acc_scr[...] = alpha * acc_scr[...] + weighted_values(p.astype(BF16))
        m_scr[...] = m_new

    own_kh = (lax.broadcasted_iota(jnp.int32, (n_rows, page * DIFF_HEADS), 1) % DIFF_HEADS
              == row_head((n_rows, page * DIFF_HEADS)))
    for j in range(pages):
        kt = k_refs[j][0].reshape(width, page)
        s = jnp.dot(qm_scr[...].astype(F32), kt, preferred_element_type=F32)

        def page_values(p, j=j):
            spread = jnp.dot(p, rep_ref[...], preferred_element_type=F32)
            mine = jnp.where(own_kh, spread, 0.0).astype(BF16)
            return jnp.dot(mine.astype(F32), v_refs[j][0], preferred_element_type=F32)
        update(s, page_values)

    @pl.when(pg == pl.num_programs(1) - 1)
    def _():
        k_new = p_ref[pl.ds(b, 1), 5 * SEG:6 * SEG].astype(BF16).astype(F32)
        s_new = jnp.sum(qm_scr[...].astype(F32) * k_new, axis=-1, keepdims=True)
        v_row = p_ref[pl.ds(b, 1), 6 * SEG:7 * SEG].astype(BF16).astype(F32)
        own_he = (lax.broadcasted_iota(jnp.int32, (n_rows, DIFF_HEADS * DIFF_DV), 1) // DIFF_DV
                  == row_head((n_rows, DIFF_HEADS * DIFF_DV)))
        v_own = jnp.where(own_he, v_row, 0.0)
        v_new = functools.reduce(jnp.add, [v_own[:, h * DIFF_DV:(h + 1) * DIFF_DV]
                                           for h in range(DIFF_HEADS)])
        update(s_new, lambda p: p.astype(F32) * v_new)
        lam = _diff_lambda(dl_ref, lam_init)
        o = acc_scr[...] / l_scr[...]
        o_ref[0] = o[0:DIFF_HEADS] - lam * o[DIFF_HEADS:n_rows]


def _diff_step(proj, k_pages_t, v_pages, page_table, dlam, lam_init):
    db = proj.shape[0]
    n_pages = page_table.shape[1]
    page = k_pages_t.shape[-1]
    pages = PAGES_PER_STEP if n_pages % PAGES_PER_STEP == 0 else 1
    rep = np.zeros((page, page * DIFF_HEADS), np.float32)
    rep[np.arange(page * DIFF_HEADS) // DIFF_HEADS, np.arange(page * DIFF_HEADS)] = 1.0
    rep = jnp.asarray(rep, BF16)
    page_of = lambda j: (lambda b, p, pt: (pt[b * n_pages + p * pages + j],))
    k_spec = lambda j: pl.BlockSpec((1,) + k_pages_t.shape[1:], lambda b, p, pt: page_of(j)(b, p, pt) + (0, 0, 0, 0))
    v_spec = lambda j: pl.BlockSpec((1,) + v_pages.shape[1:], lambda b, p, pt: page_of(j)(b, p, pt) + (0, 0))
    grid_spec = pltpu.PrefetchScalarGridSpec(
        num_scalar_prefetch=1,
        grid=(db, n_pages // pages),
        in_specs=([pl.BlockSpec(proj.shape, lambda b, p, pt: (0, 0))]
                  + [k_spec(j) for j in range(pages)] + [v_spec(j) for j in range(pages)]
                  + [pl.BlockSpec(dlam.shape, lambda b, p, pt: (0, 0)),
                     pl.BlockSpec(rep.shape, lambda b, p, pt: (0, 0))]),
        out_specs=pl.BlockSpec((1, DIFF_HEADS, DIFF_DV), lambda b, p, pt: (b, 0, 0)),
        scratch_shapes=[pltpu.VMEM((2 * DIFF_HEADS, DIFF_HEADS * 2 * DIFF_DK), BF16),
                        pltpu.VMEM((2 * DIFF_HEADS, 1), F32),
                        pltpu.VMEM((2 * DIFF_HEADS, 1), F32),
                        pltpu.VMEM((2 * DIFF_HEADS, DIFF_DV), F32)])
    return pl.pallas_call(
        functools.partial(_diff_step_kernel, lam_init=lam_init, pages=pages),
        grid_spec=grid_spec,
        out_shape=jax.ShapeDtypeStruct((db, DIFF_HEADS, DIFF_DV), F32),
        compiler_params=_cparams("parallel", "arbitrary"),
        name="diff_attn_step",
    )(page_table.reshape(-1), proj, *([k_pages_t] * pages), *([v_pages] * pages), dlam, rep)


def _merge_kernel(ro_ref, rg_ref, do_ref, sub_ref, x_ref, w_ref, o_ref, cat_scr, *, lam_init):
    for h in range(RET_HEADS):
        cols = slice(h * RET_DV, (h + 1) * RET_DV)
        g = rg_ref[:, cols]
        swish = g * (1.0 / (1.0 + jnp.exp(-g)))
        cat_scr[:, cols] = (_rms_rows(ro_ref[:, cols]) * swish).astype(BF16)
    base = RET_HEADS * RET_DV
    for h in range(DIFF_HEADS):
        cols = slice(h * DIFF_DV, (h + 1) * DIFF_DV)
        d = _rms_rows(do_ref[:, cols]) * sub_ref[...]
        cat_scr[:, base + h * DIFF_DV:base + (h + 1) * DIFF_DV] = (d * (1.0 - lam_init)).astype(BF16)
    o_ref[...] = x_ref[...] + jnp.dot(cat_scr[...], w_ref[...], preferred_element_type=F32)


def _merge_out(ro, proj, do, subln, x, w_bf16, lam_init, name):
    t, d = x.shape
    tm = min(ROW_TILE // 2, t)
    half = pl.BlockSpec((tm, SEG), lambda i: (i, 0))
    return pl.pallas_call(
        functools.partial(_merge_kernel, lam_init=lam_init),
        grid=(t // tm,),
        in_specs=[half,
                  pl.BlockSpec((tm, SEG), lambda i: (i, 3)),
                  half,
                  pl.BlockSpec((1, DIFF_DV), lambda i: (0, 0)),
                  pl.BlockSpec((tm, d), lambda i: (i, 0)),
                  pl.BlockSpec(w_bf16.shape, lambda i: (0, 0))],
        out_specs=pl.BlockSpec((tm, d), lambda i: (i, 0)),
        out_shape=jax.ShapeDtypeStruct((t, d), F32),
        scratch_shapes=[pltpu.VMEM((tm, d), BF16)],
        compiler_params=_cparams("parallel"),
        name=name,
    )(ro, proj, do, subln.reshape(1, DIFF_DV), x, w_bf16)


def _cross_heads(q, mk_ref, mv_ref, write):
    dh = q.shape[-1] // MEM_HEADS
    for h in range(MEM_HEADS):
        cols = slice(h * dh, (h + 1) * dh)
        s = lax.dot_general(q[:, cols].astype(BF16), mk_ref[0, :, cols].astype(BF16),
                            (((1,), (1,)), ((), ())), preferred_element_type=F32) * (dh ** -0.5)
        e = jnp.exp(s - jnp.max(s, axis=-1, keepdims=True))
        p = e / jnp.sum(e, axis=-1, keepdims=True)
        write(cols, jnp.dot(p.astype(BF16), mv_ref[0, :, cols].astype(BF16), preferred_element_type=F32))


def _cross_kernel(q_ref, mk_ref, mv_ref, o_ref):
    def write(cols, val):
        o_ref[:, cols] = val
    _cross_heads(q_ref[...], mk_ref, mv_ref, write)


def _cross_prompt(q, mk, mv, batch, seq):
    d = q.shape[1]
    tq = min(ROW_TILE, seq)
    n = seq // tq
    mem = pl.BlockSpec((1,) + mk.shape[1:], lambda b, i: (b, 0, 0))
    return pl.pallas_call(
        _cross_kernel,
        grid=(batch, n),
        in_specs=[pl.BlockSpec((tq, d), lambda b, i: (b * n + i, 0)), mem, mem],
        out_specs=pl.BlockSpec((tq, d), lambda b, i: (b * n + i, 0)),
        out_shape=jax.ShapeDtypeStruct(q.shape, F32),
        compiler_params=_cparams("parallel", "parallel"),
        name="cross_attn_prompt",
    )(q, mk, mv)


def _cross_step_kernel(q_ref, mk_ref, mv_ref, o_ref):
    b = pl.program_id(0)
    q = jnp.broadcast_to(q_ref[pl.ds(b, 1), :], (SUBLANES, q_ref.shape[1]))

    def write(cols, val):
        o_ref[0, :, cols] = val[0:1]
    _cross_heads(q, mk_ref, mv_ref, write)


def _cross_step(q, mk, mv):
    db, d = q.shape
    mem = pl.BlockSpec((1,) + mk.shape[1:], lambda b: (b, 0, 0))
    return pl.pallas_call(
        _cross_step_kernel,
        grid=(db,),
        in_specs=[pl.BlockSpec(q.shape, lambda b: (0, 0)), mem, mem],
        out_specs=pl.BlockSpec((1, 1, d), lambda b: (b, 0, 0)),
        out_shape=jax.ShapeDtypeStruct((db, 1, d), F32),
        compiler_params=_cparams("parallel"),
        name="cross_attn_step",
    )(q, mk, mv)


def _top_rows(x, k):
    rows = lax.broadcasted_iota(jnp.int32, x.shape, 0).astype(F32)
    n_rows = float(x.shape[0])
    vals, idxs = [], []
    for _ in range(k):
        m = jnp.max(x, axis=0, keepdims=True)
        idx = jnp.min(jnp.where(x == m, rows, n_rows), axis=0, keepdims=True)
        vals.append(m)
        idxs.append(idx)
        x = jnp.where(rows == idx, -jnp.inf, x)
    return vals, idxs


def _topk_kernel(q_ref, keys_ref, eid_ref, gate_ref):
    qb = q_ref[...].astype(BF16)
    k = PEER_TOPK
    eids, gates = [], []
    for h in range(PEER_HEADS):
        sv, si = [], []
        for c in range(2):
            col = (2 * h + c) * LANES
            st = lax.dot_general(keys_ref[2 * h + c], qb[:, col:col + LANES], (((1,), (1,)), ((), ())),
                                 preferred_element_type=F32)
            vals, idxs = _top_rows(st, k)
            sv.append(vals)
            si.append(idxs)
        v1 = jnp.concatenate(sv[1], axis=0)
        i1 = jnp.concatenate(si[1], axis=0)
        cand = jnp.concatenate([sv[0][a] + v1 for a in range(k)], axis=0)
        cidx = jnp.concatenate([si[0][a] * float(PEER_NKEYS) + i1 for a in range(k)], axis=0)
        rows = lax.broadcasted_iota(jnp.int32, cand.shape, 0).astype(F32)
        fs, fp = _top_rows(cand, k)
        eids += [jnp.max(jnp.where(rows == p, cidx, -1.0), axis=0, keepdims=True) for p in fp]
        fs = jnp.concatenate(fs, axis=0)
        e = jnp.exp(fs - jnp.max(fs, axis=0, keepdims=True))
        gates.append(e / jnp.sum(e, axis=0, keepdims=True))
    eid_t = jnp.concatenate(eids, axis=0)
    gate_t = jnp.concatenate(gates, axis=0)
    eid_ref[...] = jnp.transpose(eid_t).astype(jnp.int32)
    gate_ref[...] = jnp.transpose(gate_t)


def _peer_topk(qp, keys_bf16, name):
    t, d = qp.shape
    tm = TOPK_TOKENS
    assert t % tm == 0
    width = PEER_HEADS * PEER_TOPK
    out = pl.BlockSpec((tm, width), lambda i: (i, 0))
    return pl.pallas_call(
        _topk_kernel,
        grid=(t // tm,),
        in_specs=[pl.BlockSpec((tm, d), lambda i: (i, 0)),
                  pl.BlockSpec(keys_bf16.shape, lambda i: (0, 0, 0))],
        out_specs=[out, out],
        out_shape=[jax.ShapeDtypeStruct((t, width), jnp.int32), jax.ShapeDtypeStruct((t, width), F32)],
        compiler_params=_cparams("parallel"),
        name=name,
    )(qp, keys_bf16)


def _peer_kernel(eid_ab_ref, eid_c_ref, x_ref, gate_ref, g3_ref, gf_ref, exp_ref, msk_ref, uv_hbm, y_ref,
                 buf, sem, *, tokens):
    i = pl.program_id(0)
    n = pl.num_programs(0)
    n_exp, rows = buf.shape[2], buf.shape[4]
    width = n_exp * rows

    def issue(eid_ref, row0, slot):
        for t in range(tokens):
            for k in range(n_exp):
                pltpu.make_async_copy(uv_hbm.at[eid_ref[row0 + t, k]], buf.at[slot, t, k],
                                      sem.at[slot]).start(priority=k % 2)

    def wait(slot):
        pltpu.make_async_copy(buf.at[slot], buf.at[slot], sem.at[slot]).wait()

    def mix(row0, slot):
        x = x_ref[row0:row0 + tokens]
        ssq = jnp.sum(jnp.sum(x * x, axis=2, keepdims=True), axis=1, keepdims=True)
        hn = x * lax.rsqrt(ssq / float(rows * LANES) + EPS) * g3_ref[...]
        msk = msk_ref[...]
        z_rows = []
        for t in range(tokens):
            u2d = buf[slot, t, :, 0].reshape(width, LANES)
            zt = lax.dot_general(hn[t].astype(BF16), u2d, (((1,), (1,)), ((), ())),
                                 preferred_element_type=F32)
            z_rows.append(jnp.sum(zt * msk, axis=0, keepdims=True))
        z = jnp.concatenate(z_rows, axis=0)
        lane = lax.broadcasted_iota(jnp.int32, (tokens, LANES), 1)
        acts = []
        for c in range(width // LANES):
            zc = z[:, c * LANES:(c + 1) * LANES]
            for s in (1, 2, 4, 8):
                zc = zc + jnp.where((lane & s) == 0, pltpu.roll(zc, LANES - s, 1), pltpu.roll(zc, s, 1))
            acts.append(zc)
        act = jnp.concatenate(acts, axis=1)
        g = gate_ref[row0:row0 + tokens]
        g_hi = g.astype(BF16)
        g_lo = (g - g_hi.astype(F32)).astype(BF16)
        g_rep = (jnp.dot(g_hi, exp_ref[...], preferred_element_type=F32)
                 + jnp.dot(g_lo, exp_ref[...], preferred_element_type=F32))
        w = g_rep * (0.5 * act * (1.0 + lax.erf(act * (2.0 ** -0.5))))
        for t in range(tokens):
            w_sel = (msk * w[t:t + 1, :]).astype(BF16)
            v2d = buf[slot, t, :, 1].reshape(width, LANES)
            xo = x[t] + jnp.dot(w_sel, v2d, preferred_element_type=F32)
            ms = jnp.sum(jnp.sum(xo * xo, axis=1, keepdims=True), axis=0, keepdims=True)
            y_ref[row0 + t] = xo * lax.rsqrt(ms / float(rows * LANES) + EPS) * gf_ref[...]

    @pl.when(i == 0)
    def _():
        issue(eid_ab_ref, 0, 0)

    issue(eid_ab_ref, tokens, 1)
    wait(0)
    mix(0, 0)

    @pl.when(i + 1 < n)
    def _():
        issue(eid_c_ref, 0, 0)

    wait(1)
    mix(tokens, 1)


def _peer_mix(x, eid, gate, g3, gf, uv_slab, name):
    t, d = x.shape
    tokens = PEER_TOKENS
    assert t % (2 * tokens) == 0
    n = t // (2 * tokens)
    rows = d // LANES
    slab = (rows, LANES)
    n_exp = PEER_HEADS * PEER_TOPK
    width = n_exp * rows
    r = np.arange(width)
    expand = jnp.asarray((r[None, :] // rows) == np.arange(n_exp)[:, None], BF16)
    mask = jnp.asarray((r[None, :] % rows) == np.arange(rows)[:, None], F32)
    x3 = x.reshape(t, *slab)
    tok_spec = pl.BlockSpec((2 * tokens,) + slab, lambda i: (i, 0, 0))
    const = lambda a: pl.BlockSpec(a.shape, lambda i: (0, 0))
    y = pl.pallas_call(
        functools.partial(_peer_kernel, tokens=tokens),
        grid=(n,),
        in_specs=[pl.BlockSpec((2 * tokens, n_exp), lambda i: (i, 0), memory_space=pltpu.SMEM),
                  pl.BlockSpec((tokens, n_exp), lambda i: (jnp.minimum(2 * i + 2, 2 * n - 1), 0),
                               memory_space=pltpu.SMEM),
                  tok_spec,
                  pl.BlockSpec((2 * tokens, n_exp), lambda i: (i, 0)),
                  pl.BlockSpec(slab, lambda i: (0, 0)),
                  pl.BlockSpec(slab, lambda i: (0, 0)),
                  const(expand), const(mask),
                  pl.BlockSpec(memory_space=pl.ANY)],
        out_specs=tok_spec,
        out_shape=jax.ShapeDtypeStruct(x3.shape, F32),
        scratch_shapes=[pltpu.VMEM((2, tokens, n_exp, 2) + slab, BF16),
                        pltpu.SemaphoreType.DMA((2,))],
        compiler_params=_cparams("arbitrary"),
        name=name,
    )(eid, eid, x3, gate, g3.reshape(slab), gf.reshape(slab), expand, mask, uv_slab)
    return y.reshape(t, d)


def _peer(x, g3, gf, w_pq, keys_bf16, uv_slab, name):
    t = x.shape[0]
    qp = _proj(x, w_pq, gain=g3, name=name + "_query")
    pad = (-t) % TOPK_TOKENS
    if pad:
        qp = jnp.pad(qp, ((0, pad), (0, 0)))
    eid, gate = _peer_topk(qp, keys_bf16, name + "_topk")
    return _peer_mix(x, eid[:t], gate[:t], g3, gf, uv_slab, name + "_mix")


def kernel(x_prompt, x_sample, state_ret, cache_diff_k, cache_diff_v, cache_mem_k, cache_mem_v, page_table, mem_prompt, norm1, w_in, diff_lambda, diff_subln, w_out, norm2, norm_mem, w_cq, w_ck, w_cv, w_co, norm3, w_pq, peer_sub_keys, peer_u, peer_v, norm_f):
    B, L, D = x_prompt.shape
    DB, TS = x_sample.shape[:2]
    assert w_in.shape[0] == 1 and TS == 1
    M = mem_prompt.shape[1]
    n_pool, page = cache_diff_k.shape[1:3]
    past = page_table.shape[1] * page
    n_experts = peer_u.shape[1]
    slab = (D // LANES, LANES)

    xp = x_prompt.reshape(B * L, D)
    xs = x_sample.reshape(DB * TS, D)
    tab_p = _rope_tables(jnp.arange(L, dtype=F32))
    tab_s = _rope_tables(jnp.broadcast_to(jnp.arange(TS, dtype=F32) + past, (DB * TS,)))

    layer = 0
    lam_init = 0.8 - 0.6 * math.exp(-0.3 * layer)
    w_in_b = w_in[layer].astype(BF16)
    w_out_b = w_out[layer].astype(BF16)
    w_cq_b, w_ck_b, w_cv_b, w_co_b = (w[layer].astype(BF16) for w in (w_cq, w_ck, w_cv, w_co))
    w_pq_b = w_pq[layer].astype(BF16)
    keys_b = peer_sub_keys[layer].reshape(2 * PEER_HEADS, PEER_NKEYS, -1).astype(BF16)
    uv_slab = jnp.stack([peer_u[layer].astype(BF16).reshape(n_experts, *slab),
                         peer_v[layer].astype(BF16).reshape(n_experts, *slab)], axis=1)
    dlam = diff_lambda[layer]
    g1, g2, g3, gm = norm1[layer], norm2[layer], norm3[layer], norm_mem[layer]

    proj_s = _inproj(xs, g1, w_in_b, tab_s, "in_proj_sample")
    ro_s, s_new = _retention_step(proj_s, state_ret[layer])
    k_pages_t = jnp.transpose(cache_diff_k[layer], (0, 2, 3, 4, 1))
    v_pages = cache_diff_v[layer].reshape(n_pool, page * DIFF_HEADS, DIFF_DV)
    do_s = _diff_step(proj_s, k_pages_t, v_pages, page_table, dlam, lam_init)
    xs = _merge_out(ro_s.reshape(DB, -1), proj_s, do_s.reshape(DB, -1), diff_subln[layer], xs, w_out_b,
                    lam_init, "merge_out_sample")
    q_s = _proj(xs, w_cq_b, gain=g2, name="cross_q_sample")
    o_s = _cross_step(q_s, cache_mem_k[layer].reshape(DB, M, D), cache_mem_v[layer].reshape(DB, M, D))
    xs = _proj(o_s.reshape(DB, D), w_co_b, residual=xs, name="cross_o_sample")
    y_s = _peer(xs, g3, norm_f, w_pq_b, keys_b, uv_slab, "peer_sample")

    proj_p = _inproj(xp, g1, w_in_b, tab_p, "in_proj_prompt")
    ro_p, sfin_p = _retention_prompt(proj_p, B, L)
    do_p = _diff_prompt(proj_p, dlam, B, L, lam_init)
    xp = _merge_out(ro_p, proj_p, do_p, diff_subln[layer], xp, w_out_b, lam_init, "merge_out_prompt")
    mem = mem_prompt.reshape(B * M, D)
    mk_p = _proj(mem, w_ck_b, gain=gm, name="mem_k")
    mv_p = _proj(mem, w_cv_b, gain=gm, name="mem_v")
    q_p = _proj(xp, w_cq_b, gain=g2, name="cross_q_prompt")
    o_p = _cross_prompt(q_p, mk_p.reshape(B, M, D), mv_p.reshape(B, M, D), B, L)
    xp = _proj(o_p, w_co_b, residual=xp, name="cross_o_prompt")
    y_p = _peer(xp, g3, norm_f, w_pq_b, keys_b, uv_slab, "peer_prompt")

    seg = lambda p, s: p[:, s * SEG:(s + 1) * SEG]
    return (y_p.reshape(B, L, D),
            y_s.reshape(DB, TS, D),
            sfin_p[None],
            seg(proj_p, 5).reshape(1, B, L, DIFF_HEADS, 2, DIFF_DK),
            seg(proj_p, 6).reshape(1, B, L, DIFF_HEADS, DIFF_DV),
            mk_p.reshape(1, B, M, MEM_HEADS, D // MEM_HEADS),
            mv_p.reshape(1, B, M, MEM_HEADS, D // MEM_HEADS),
            s_new[None],
            seg(proj_s, 5).reshape(1, DB, TS, DIFF_HEADS, 2, DIFF_DK),
            seg(proj_s, 6).reshape(1, DB, TS, DIFF_HEADS, DIFF_DV))
```
